```python
import math
import jax, jax.numpy as jnp
from jax import lax
import numpy as np

D_MODEL = 1024
BATCH = 4
SEQ = 8192
DEPTH = 1

GDN_HEADS = 4
GDN_DK = 128
GDN_DV = 128
GDN_CONV = 4
GDN_CHUNK = 64
DIFF_HEADS = 4
DIFF_DH = 64
Q_BLOCK = 128
REL_BUCKETS = 32
REL_MAX_DIST = 128
N_EXPERTS = 32
TOP_K = 4
D_EXPERT = 1024
SWIGLU_LIMIT = 7.0
SWIGLU_ALPHA = 1.702
EXPERT_BLOCK = 256
PLE_DIM = 256
DEEPNORM_ALPHA = (2 * DEPTH) ** 0.25
DEEPNORM_BETA = (8 * DEPTH) ** -0.25
LN_EPS = 1e-5
RMS_EPS = 1e-6

GDN_QK_W = GDN_HEADS * GDN_DK
GDN_V_W = GDN_HEADS * GDN_DV
DIFF_QK_W = DIFF_HEADS * 2 * DIFF_DH
DIFF_V_W = DIFF_HEADS * 2 * DIFF_DH
MIX_W = GDN_V_W + DIFF_V_W
IN_SPLITS = (GDN_QK_W, GDN_QK_W, GDN_V_W, GDN_HEADS, GDN_HEADS, GDN_V_W, DIFF_QK_W, DIFF_QK_W, DIFF_V_W)
D_IN_PROJ = sum(IN_SPLITS)

kernel_name = 'hymba_gdn_diffattn_moe_deepnorm'


def split_cols(t, sizes):
    offs = []
    acc = 0
    for s in sizes[:-1]:
        acc += s
        offs.append(acc)
    return jnp.split(t, offs, axis=-1)


def layer_norm(x, g, b):
    xf = x.astype(jnp.float32)
    mu = jnp.mean(xf, axis=-1, keepdims=True)
    var = jnp.mean(jnp.square(xf - mu), axis=-1, keepdims=True)
    y = (xf - mu) * lax.rsqrt(var + LN_EPS) * g.astype(jnp.float32) + b.astype(jnp.float32)
    return y.astype(x.dtype)


def rms_norm(x, w):
    xf = x.astype(jnp.float32)
    return xf * lax.rsqrt(jnp.mean(xf * xf, axis=-1, keepdims=True) + RMS_EPS) * w.astype(jnp.float32)


def l2_normalize(t):
    return t * lax.rsqrt(jnp.sum(t * t, axis=-1, keepdims=True) + 1e-6)


def causal_depthwise_conv(x, w):
    K, C = w.shape
    return lax.conv_general_dilated(x, w[:, None, :].astype(x.dtype), window_strides=(1,),
                                    padding=[(K - 1, 0)], dimension_numbers=('NWC', 'WIO', 'NWC'),
                                    feature_group_count=C)


def t5_bucket(dist):
    max_exact = REL_BUCKETS // 2
    d = jnp.maximum(dist, 0)
    large = max_exact + (jnp.log(jnp.maximum(d, 1).astype(jnp.float32) / max_exact)
                         / math.log(REL_MAX_DIST / max_exact) * (REL_BUCKETS - max_exact)).astype(jnp.int32)
    large = jnp.minimum(large, REL_BUCKETS - 1)
    return jnp.where(d < max_exact, d, large)


def gated_delta_rule_chunked(q, k, v, g, beta):
    B, S, H, dk = q.shape
    dv = v.shape[-1]
    C = GDN_CHUNK
    n = S // C
    f32 = jnp.float32

    def chunk(t):
        t = t.astype(f32).reshape((B, n, C, H) + t.shape[3:])
        return jnp.moveaxis(t, 3, 1)

    q = chunk(l2_normalize(q.astype(f32)) * dk ** -0.5)
    k = chunk(l2_normalize(k.astype(f32)))
    v = chunk(v)
    g = jnp.cumsum(chunk(g), axis=-1)
    beta = chunk(beta)
    lower = jnp.tril(jnp.ones((C, C), dtype=bool))
    strict = jnp.tril(jnp.ones((C, C), dtype=bool), k=-1)
    gdiff = g[..., :, None] - g[..., None, :]
    decay = jnp.where(lower, jnp.exp(jnp.where(lower, gdiff, 0.0)), 0.0)
    kb = k * beta[..., None]
    L = jnp.where(strict, jnp.einsum('bhnid,bhnjd->bhnij', kb, k) * decay, 0.0)
    eye = jnp.eye(C, dtype=f32)
    rhs = jnp.concatenate([v * beta[..., None], kb * jnp.exp(g)[..., None]], axis=-1)
    sol = lax.linalg.triangular_solve(eye + L, rhs, left_side=True, lower=True, unit_diagonal=True)
    u, w = sol[..., :dv], sol[..., dv:]
    intra = jnp.einsum('bhnid,bhnjd->bhnij', q, k) * decay
    q_dec = q * jnp.exp(g)[..., None]
    k_dec = k * jnp.exp(g[..., -1:] - g)[..., None]
    g_last = jnp.exp(g[..., -1])

    def step(state, inp):
        q_i, k_i, u_i, w_i, a_i, gl = inp
        v_new = u_i - jnp.einsum('bhck,bhkv->bhcv', w_i, state)
        o = jnp.einsum('bhck,bhkv->bhcv', q_i, state) + jnp.einsum('bhcj,bhjv->bhcv', a_i, v_new)
        state = state * gl[..., None, None] + jnp.einsum('bhck,bhcv->bhkv', k_i, v_new)
        return state, o

    xs = tuple(jnp.moveaxis(t, 2, 0) for t in (q_dec, k_dec, u, w, intra, g_last))
    _, o = lax.scan(step, jnp.zeros((B, H, dk, dv), f32), xs)
    return jnp.transpose(o, (1, 0, 3, 2, 4)).reshape(B, S, H, dv)


def diff_attention_blocked(q1, q2, k1, k2, v, lam, rel_bias):
    B, S, H, dh = q1.shape
    nblk = S // Q_BLOCK
    scale = dh ** -0.5
    dist_bias = rel_bias[t5_bucket(jnp.arange(S))].astype(jnp.float32)
    k_pos = jnp.arange(S)

    def to_blocks(t):
        return jnp.moveaxis(t.reshape(B, nblk, Q_BLOCK, H, t.shape[-1]), 1, 0)

    def block(args):
        bi, qa, qb = args
        q_pos = bi * Q_BLOCK + jnp.arange(Q_BLOCK)
        rel = q_pos[:, None] - k_pos[None, :]
        causal = rel >= 0
        bias = jnp.transpose(dist_bias[jnp.clip(rel, 0, S - 1)], (2, 0, 1))

        def probs(qx, kx):
            s = jnp.einsum('bqhd,bkhd->bhqk', qx, kx).astype(jnp.float32) * scale + bias
            return jax.nn.softmax(jnp.where(causal, s, -jnp.inf), axis=-1)

        a = probs(qa, k1) - lam * probs(qb, k2)
        return jnp.einsum('bhqk,bkhe->bqhe', a.astype(v.dtype), v)

    out = lax.map(block, (jnp.arange(nblk), to_blocks(q1), to_blocks(q2)))
    return jnp.moveaxis(out, 0, 1).reshape(B, S, H, v.shape[-1])


def moe_ffn(h, router_w, router_b, w_gate_up, b_gate_up, w_down, b_down):
    B, S, D = h.shape
    N = B * S
    A = N * TOP_K
    xf = h.reshape(N, D)
    logits = (xf @ router_w).astype(jnp.float32) + router_b.astype(jnp.float32)
    top_val, top_idx = lax.top_k(logits, TOP_K)
    gates = jax.nn.softmax(top_val, axis=-1)
    e_flat = top_idx.reshape(A)
    tok_flat = jnp.arange(A, dtype=jnp.int32) // TOP_K
    order = jnp.argsort(e_flat)
    e_sorted = e_flat[order]
    counts = jnp.bincount(e_flat, length=N_EXPERTS)
    padded = (counts + EXPERT_BLOCK - 1) // EXPERT_BLOCK * EXPERT_BLOCK
    start = jnp.cumsum(counts) - counts
    pend = jnp.cumsum(padded)
    pstart = pend - padded
    dest = pstart[e_sorted] + jnp.arange(A, dtype=jnp.int32) - start[e_sorted]
    n_blocks = -(-A // EXPERT_BLOCK) + N_EXPERTS
    P = n_blocks * EXPERT_BLOCK
    tok_buf = jnp.full((P,), N, dtype=jnp.int32).at[dest].set(tok_flat[order])
    gate_buf = jnp.zeros((P,), jnp.float32).at[dest].set(gates.reshape(A)[order])
    blk_expert = jnp.minimum(jnp.searchsorted(pend, jnp.arange(n_blocks) * EXPERT_BLOCK, side='right'),
                             N_EXPERTS - 1)
    x_pad = jnp.concatenate([xf, jnp.zeros((1, D), xf.dtype)], axis=0)

    def expert_block(args):
        tok_b, e = args
        gu = x_pad[tok_b] @ w_gate_up[e] + b_gate_up[e]
        gate = jnp.minimum(gu[:, :D_EXPERT], SWIGLU_LIMIT)
        up = jnp.clip(gu[:, D_EXPERT:], -SWIGLU_LIMIT, SWIGLU_LIMIT)
        act = (up + 1.0) * gate * jax.nn.sigmoid(SWIGLU_ALPHA * gate)
        return act @ w_down[e] + b_down[e]

    y_blocks = lax.map(expert_block, (tok_buf.reshape(n_blocks, EXPERT_BLOCK), blk_expert))
    y = jax.ops.segment_sum(y_blocks.reshape(P, D) * gate_buf[:, None].astype(y_blocks.dtype), tok_buf,
                            num_segments=N + 1)
    return y[:N].reshape(B, S, D)


def setup_inputs(seed: int = 0) -> dict:
    key = jax.random.key(seed)
    ks = jax.random.split(key, 29)
    f32 = jnp.float32
    nrm = lambda k, shape, s: jax.random.normal(k, shape, f32) * s
    dt = jnp.exp(jax.random.uniform(ks[5], (DEPTH, GDN_HEADS), f32, math.log(1e-3), math.log(1e-1)))
    return {
        'x': nrm(ks[0], (BATCH, SEQ, D_MODEL), 1.0),
        'p': nrm(ks[1], (DEPTH, BATCH, SEQ, PLE_DIM), 1.0),
        'w_in': nrm(ks[2], (DEPTH, D_MODEL, D_IN_PROJ), D_MODEL ** -0.5),
        'conv_w': nrm(ks[3], (DEPTH, GDN_CONV, 2 * GDN_QK_W + GDN_V_W), GDN_CONV ** -0.5),
        'gdn_a_log': jnp.log(jax.random.uniform(ks[4], (DEPTH, GDN_HEADS), f32, 1.0, 16.0)),
        'gdn_dt_bias': dt + jnp.log(-jnp.expm1(-dt)),
        'gdn_norm_w': 1.0 + nrm(ks[6], (DEPTH, GDN_DV), 0.02),
        'diff_lq1': nrm(ks[7], (DEPTH, DIFF_DH), 0.1),
        'diff_lk1': nrm(ks[8], (DEPTH, DIFF_DH), 0.1),
        'diff_lq2': nrm(ks[9], (DEPTH, DIFF_DH), 0.1),
        'diff_lk2': nrm(ks[10], (DEPTH, DIFF_DH), 0.1),
        'diff_norm_w': 1.0 + nrm(ks[11], (DEPTH, 2 * DIFF_DH), 0.02),
        'w_out': nrm(ks[12], (DEPTH, MIX_W, D_MODEL), MIX_W ** -0.5 * DEEPNORM_BETA),
        'rel_bias': nrm(ks[13], (REL_BUCKETS, DIFF_HEADS), 0.3),
        'ln1_g': 1.0 + nrm(ks[14], (DEPTH, D_MODEL), 0.02),
        'ln1_b': nrm(ks[15], (DEPTH, D_MODEL), 0.02),
        'router_w': nrm(ks[16], (DEPTH, D_MODEL, N_EXPERTS), D_MODEL ** -0.5),
        'router_b': nrm(ks[17], (DEPTH, N_EXPERTS), 0.01),
        'w_gate_up': nrm(ks[18], (DEPTH, N_EXPERTS, D_MODEL, 2 * D_EXPERT), D_MODEL ** -0.5),
        'b_gate_up': nrm(ks[19], (DEPTH, N_EXPERTS, 2 * D_EXPERT), 0.01),
        'w_down': nrm(ks[20], (DEPTH, N_EXPERTS, D_EXPERT, D_MODEL), D_EXPERT ** -0.5 * DEEPNORM_BETA),
        'b_down': nrm(ks[21], (DEPTH, N_EXPERTS, D_MODEL), 0.01),
        'ln2_g': 1.0 + nrm(ks[22], (DEPTH, D_MODEL), 0.02),
        'ln2_b': nrm(ks[23], (DEPTH, D_MODEL), 0.02),
        'ple_gate_w': nrm(ks[24], (DEPTH, D_MODEL, D_MODEL), D_MODEL ** -0.5),
        'ple_gate_b': nrm(ks[25], (DEPTH, D_MODEL), 0.01),
        'ple_proj_w': nrm(ks[26], (DEPTH, PLE_DIM, D_MODEL), PLE_DIM ** -0.5 * DEEPNORM_BETA),
        'ln3_g': 1.0 + nrm(ks[27], (DEPTH, D_MODEL), 0.02),
        'ln3_b': nrm(ks[28], (DEPTH, D_MODEL), 0.02),
    }


def reference(x, p, w_in, conv_w, gdn_a_log, gdn_dt_bias, gdn_norm_w, diff_lq1, diff_lk1, diff_lq2,
              diff_lk2, diff_norm_w, w_out, rel_bias, ln1_g, ln1_b, router_w, router_b, w_gate_up,
              b_gate_up, w_down, b_down, ln2_g, ln2_b, ple_gate_w, ple_gate_b, ple_proj_w, ln3_g, ln3_b):
    B, S, _ = x.shape
    f32 = jnp.float32
    for i in range(DEPTH):
        lam_init = 0.8 - 0.6 * math.exp(-0.3 * i)
        proj = x @ w_in[i]
        gq, gk, gv, ga, gb, gz, dq, dk, dv = split_cols(proj, IN_SPLITS)
        qkv = jax.nn.silu(causal_depthwise_conv(jnp.concatenate([gq, gk, gv], axis=-1), conv_w[i]))
        gq, gk, gv = split_cols(qkv, (GDN_QK_W, GDN_QK_W, GDN_V_W))
        g_log = -jnp.exp(gdn_a_log[i].astype(f32)) * jax.nn.softplus(ga.astype(f32) + gdn_dt_bias[i].astype(f32))
        beta = jax.nn.sigmoid(gb.astype(f32))
        o_gdn = gated_delta_rule_chunked(gq.reshape(B, S, GDN_HEADS, GDN_DK), gk.reshape(B, S, GDN_HEADS, GDN_DK),
                                         gv.reshape(B, S, GDN_HEADS, GDN_DV), g_log, beta)
        o_gdn = rms_norm(o_gdn, gdn_norm_w[i]) * jax.nn.silu(gz.reshape(B, S, GDN_HEADS, GDN_DV).astype(f32))
        dq = dq.reshape(B, S, DIFF_HEADS, 2, DIFF_DH)
        dk = dk.reshape(B, S, DIFF_HEADS, 2, DIFF_DH)
        lam = (jnp.exp(jnp.sum(diff_lq1[i].astype(f32) * diff_lk1[i].astype(f32)))
               - jnp.exp(jnp.sum(diff_lq2[i].astype(f32) * diff_lk2[i].astype(f32))) + lam_init)
        o_diff = diff_attention_blocked(dq[:, :, :, 0], dq[:, :, :, 1], dk[:, :, :, 0], dk[:, :, :, 1],
                                        dv.reshape(B, S, DIFF_HEADS, 2 * DIFF_DH), lam, rel_bias)
        o_diff = rms_norm(o_diff, diff_norm_w[i]) * (1.0 - lam_init)
        heads = jnp.concatenate([o_gdn.reshape(B, S, GDN_V_W), o_diff.reshape(B, S, DIFF_V_W)], axis=-1)
        mix = heads.astype(x.dtype) @ w_out[i]
        x = layer_norm(DEEPNORM_ALPHA * x + mix, ln1_g[i], ln1_b[i])
        ffn = moe_ffn(x, router_w[i], router_b[i], w_gate_up[i], b_gate_up[i], w_down[i], b_down[i])
        x = layer_norm(DEEPNORM_ALPHA * x + ffn, ln2_g[i], ln2_b[i])
        gate = jax.nn.sigmoid(x @ ple_gate_w[i] + ple_gate_b[i])
        x = layer_norm(DEEPNORM_ALPHA * x + gate * (p[i] @ ple_proj_w[i]), ln3_g[i], ln3_b[i])
    return x
```

```python
import functools
import math

import jax
import jax.numpy as jnp
from jax import lax
from jax.experimental import pallas as pl
from jax.experimental.pallas import tpu as pltpu

f32 = jnp.float32
bf16 = jnp.bfloat16
i32 = jnp.int32

GDN_HEADS = 4
GDN_DK = 128
GDN_DV = 128
GDN_CONV = 4
GDN_CHUNK = 64
DIFF_HEADS = 4
DIFF_DH = 64
REL_BUCKETS = 32
REL_MAX_DIST = 128
N_EXPERTS = 32
TOP_K = 4
D_EXPERT = 1024
SWIGLU_LIMIT = 7.0
SWIGLU_ALPHA = 1.702
LN_EPS = 1e-5
RMS_EPS = 1e-6

GDN_W = GDN_HEADS * GDN_DK
DIFF_W = DIFF_HEADS * 2 * DIFF_DH
AB_W = 128
PROJ_W = 3 * GDN_W + GDN_W + 3 * DIFF_W + AB_W

VMEM_LIMIT = 56 * 1024 * 1024
NEG_BIG = -1e30


def _tile(n, want):
    t = min(n, want)
    assert n % t == 0, (n, t)
    return t


def _in_proj_kernel(x_ref, w_ref, gqkv_ref, z_ref, dqkv_ref, ab_ref):
    xb = x_ref[...].astype(bf16)

    def mm(c0, width):
        return jnp.dot(xb, w_ref[:, c0:c0 + width], preferred_element_type=f32)

    for j in range(3):
        gqkv_ref[:, j * GDN_W:(j + 1) * GDN_W] = mm(j * GDN_W, GDN_W).astype(bf16)
    z_ref[...] = mm(3 * GDN_W, GDN_W).astype(bf16)
    base = 4 * GDN_W
    for j in range(3):
        dqkv_ref[:, j * DIFF_W:(j + 1) * DIFF_W] = mm(base + j * DIFF_W, DIFF_W).astype(bf16)
    ab_ref[...] = mm(base + 3 * DIFF_W, AB_W)


def _in_proj(x2d, w_perm):
    n, d = x2d.shape
    tm = _tile(n, 512)
    return pl.pallas_call(
        _in_proj_kernel,
        grid=(n // tm,),
        in_specs=[pl.BlockSpec((tm, d), lambda i: (i, 0)),
                  pl.BlockSpec((d, PROJ_W), lambda i: (0, 0))],
        out_specs=[pl.BlockSpec((tm, 3 * GDN_W), lambda i: (i, 0)),
                   pl.BlockSpec((tm, GDN_W), lambda i: (i, 0)),
                   pl.BlockSpec((tm, 3 * DIFF_W), lambda i: (i, 0)),
                   pl.BlockSpec((tm, AB_W), lambda i: (i, 0))],
        out_shape=[jax.ShapeDtypeStruct((n, 3 * GDN_W), bf16),
                   jax.ShapeDtypeStruct((n, GDN_W), bf16),
                   jax.ShapeDtypeStruct((n, 3 * DIFF_W), bf16),
                   jax.ShapeDtypeStruct((n, AB_W), f32)],
        compiler_params=pltpu.CompilerParams(dimension_semantics=("arbitrary",),
                                             vmem_limit_bytes=VMEM_LIMIT),
        name="in_proj",
    )(x2d, w_perm)


def _unit_lower_inverse(l_strict):
    c = l_strict.shape[0]
    eye = (lax.broadcasted_iota(i32, (c, c), 0) == lax.broadcasted_iota(i32, (c, c), 1)).astype(f32)
    hp = dict(preferred_element_type=f32, precision=lax.Precision.HIGHEST)
    pw = -l_strict
    inv = eye + pw
    steps = int(math.log2(c)) - 1
    for _ in range(steps):
        pw = jnp.dot(pw, pw, **hp)
        inv = inv + jnp.dot(inv, pw, **hp)
    return inv


def _gdn_kernel(qkv_ref, ab_ref, z_ref, convw_ref, aexp_ref, dtb_ref, nw_ref, o_ref, xe_ref, state_ref, *, tb):
    c = GDN_CHUNK
    t = pl.program_id(1)

    @pl.when(t == 0)
    def _():
        xe_ref[0:8, :] = jnp.zeros((8, 3 * GDN_W), f32)
        state_ref[...] = jnp.zeros_like(state_ref)

    xe_ref[8:8 + tb, :] = qkv_ref[...].astype(f32)

    def conv_silu(c0):
        acc = xe_ref[5:5 + tb, c0:c0 + 128] * convw_ref[0:1, c0:c0 + 128]
        for j in range(1, GDN_CONV):
            acc = acc + xe_ref[5 + j:5 + j + tb, c0:c0 + 128] * convw_ref[j:j + 1, c0:c0 + 128]
        return acc * jax.nn.sigmoid(acc)

    def l2n(v):
        return v * lax.rsqrt(jnp.sum(v * v, axis=-1, keepdims=True) + 1e-6)

    ab = ab_ref[...]
    xg = ab + dtb_ref[...]
    softplus = jnp.maximum(xg, 0.0) + jnp.log(1.0 + jnp.exp(-jnp.abs(xg)))
    glog = -aexp_ref[...] * softplus
    beta = jax.nn.sigmoid(ab)

    row = lax.broadcasted_iota(i32, (c, c), 0)
    col = lax.broadcasted_iota(i32, (c, c), 1)
    lower = row >= col
    strict = row > col
    lower_f = lower.astype(f32)
    nw = nw_ref[...]

    for h in range(GDN_HEADS):
        qh = l2n(conv_silu(h * GDN_DK)) * (GDN_DK ** -0.5)
        kh = l2n(conv_silu(GDN_W + h * GDN_DK))
        vh = conv_silu(2 * GDN_W + h * GDN_DV)
        for ci in range(tb // c):
            r0 = ci * c
            gcum = jnp.dot(lower_f, glog[r0:r0 + c, :], preferred_element_type=f32,
                           precision=lax.Precision.HIGHEST)
            g_col = gcum[:, h:h + 1]
            g_row = jnp.transpose(gcum)[h:h + 1, :]
            b_col = beta[r0:r0 + c, GDN_HEADS + h:GDN_HEADS + h + 1]
            decay = jnp.where(lower, jnp.exp(jnp.where(lower, g_col - g_row, 0.0)), 0.0)
            eg = jnp.exp(g_col)
            g_last = gcum[c - 1:c, h:h + 1]
            q = qh[r0:r0 + c]
            k = kh[r0:r0 + c]
            v = vh[r0:r0 + c]
            kb = k * b_col
            k_b16 = k.astype(bf16)
            kk = lax.dot_general(kb.astype(bf16), k_b16, (((1,), (1,)), ((), ())), preferred_element_type=f32)
            tinv = _unit_lower_inverse(jnp.where(strict, kk * decay, 0.0))
            rhs = jnp.concatenate([v * b_col, kb * eg], axis=1)
            sol = jnp.dot(tinv.astype(bf16), rhs.astype(bf16), preferred_element_type=f32)
            u = sol[:, :GDN_DV]
            w = sol[:, GDN_DV:]
            qk = lax.dot_general(q.astype(bf16), k_b16, (((1,), (1,)), ((), ())), preferred_element_type=f32)
            intra = jnp.where(lower, qk * decay, 0.0)
            q_dec = q * eg
            k_dec = k * jnp.exp(g_last - g_col)
            s = state_ref[h]
            wq = jnp.concatenate([w, q_dec], axis=0).astype(bf16)
            r = jnp.dot(wq, s.astype(bf16), preferred_element_type=f32)
            v_new = u - r[:c]
            v_new_b = v_new.astype(bf16)
            o = r[c:] + jnp.dot(intra.astype(bf16), v_new_b, preferred_element_type=f32)
            state_ref[h] = s * jnp.exp(g_last) + lax.dot_general(
                k_dec.astype(bf16), v_new_b, (((0,), (0,)), ((), ())), preferred_element_type=f32)
            zg = z_ref[r0:r0 + c, h * GDN_DV:(h + 1) * GDN_DV].astype(f32)
            on = o * lax.rsqrt(jnp.mean(o * o, axis=-1, keepdims=True) + RMS_EPS) * nw
            o_ref[r0:r0 + c, h * GDN_DV:(h + 1) * GDN_DV] = (on * (zg * jax.nn.sigmoid(zg))).astype(bf16)

    xe_ref[0:8, :] = xe_ref[tb:tb + 8, :]


def _gdn(gqkv, ab, z, conv_w, a_exp, dt_bias, norm_w, batch, seq):
    tb = _tile(seq, 128)
    nt = seq // tb
    row_map = lambda b, t: (b * nt + t, 0)
    const = lambda b, t: (0, 0)
    return pl.pallas_call(
        functools.partial(_gdn_kernel, tb=tb),
        grid=(batch, nt),
        in_specs=[pl.BlockSpec((tb, 3 * GDN_W), row_map),
                  pl.BlockSpec((tb, AB_W), row_map),
                  pl.BlockSpec((tb, GDN_W), row_map),
                  pl.BlockSpec((GDN_CONV, 3 * GDN_W), const),
                  pl.BlockSpec((1, AB_W), const),
                  pl.BlockSpec((1, AB_W), const),
                  pl.BlockSpec((1, GDN_DV), const)],
        out_specs=pl.BlockSpec((tb, GDN_W), row_map),
        out_shape=jax.ShapeDtypeStruct((batch * seq, GDN_W), bf16),
        scratch_shapes=[pltpu.VMEM((tb + 8, 3 * GDN_W), f32),
                        pltpu.VMEM((GDN_HEADS, GDN_DK, GDN_DV), f32)],
        compiler_params=pltpu.CompilerParams(dimension_semantics=("arbitrary", "arbitrary"),
                                             vmem_limit_bytes=VMEM_LIMIT),
        name="gdn",
    )(gqkv, ab, z, conv_w, a_exp, dt_bias, norm_w)


def _diff_attn_kernel(qi_ref, kj_ref, q_ref, k_ref, v_ref, bias0_ref, bias1_ref, nw_ref, lam_ref, o_ref,
                      qbd_ref, m_ref, l_ref, acc_ref, *, tq, lam_init):
    step = pl.program_id(2)
    qi = qi_ref[step]
    kj = kj_ref[step]

    @pl.when(kj == 0)
    def _():
        q = q_ref[...] * (DIFF_DH ** -0.5)
        first = lax.broadcasted_iota(i32, q.shape, 1) < DIFF_DH
        zero = jnp.zeros_like(q)
        qbd_ref[0:tq, :] = jnp.where(first, q, zero)
        qbd_ref[tq:2 * tq, :] = jnp.where(first, zero, q)
        m_ref[...] = jnp.full_like(m_ref, NEG_BIG)
        l_ref[...] = jnp.zeros_like(l_ref)
        acc_ref[...] = jnp.zeros_like(acc_ref)

    def update(bias_ref):
        s = lax.dot_general(k_ref[...], qbd_ref[...], (((1,), (1,)), ((), ())), preferred_element_type=f32)
        if bias_ref is not None:
            b = bias_ref[0]
            s = s + jnp.concatenate([b, b], axis=1)
        m_prev = m_ref[...]
        m_new = jnp.maximum(m_prev, jnp.max(s, axis=0, keepdims=True))
        alpha = jnp.exp(m_prev - m_new)
        p = jnp.exp(s - m_new)
        l_ref[...] = alpha * l_ref[...] + jnp.sum(p, axis=0, keepdims=True)
        pv = lax.dot_general(v_ref[...], p.astype(bf16), (((0,), (0,)), ((), ())), preferred_element_type=f32)
        acc_ref[...] = alpha * acc_ref[...] + pv
        m_ref[...] = m_new

    @pl.when(kj == qi)
    def _():
        update(bias0_ref)

    @pl.when(kj == qi - 1)
    def _():
        update(bias1_ref)

    @pl.when(kj < qi - 1)
    def _():
        update(None)

    @pl.when(kj == qi)
    def _():
        o = acc_ref[...] * (1.0 / l_ref[...])
        d = o[:, :tq] - lam_ref[0, 0] * o[:, tq:]
        ms = jnp.mean(d * d, axis=0, keepdims=True)
        y = d * lax.rsqrt(ms + RMS_EPS) * (nw_ref[...] * (1.0 - lam_init))
        o_ref[...] = jnp.transpose(y).astype(bf16)


def _diff_attn(dqkv, bias0, bias1, norm_w_col, lam, batch, seq, lam_init):
    tq = _tile(seq, 512)
    nq = seq // tq
    steps = [(qi, kj) for qi in range(nq) for kj in range(qi + 1)]
    qi_tab = jnp.asarray([s[0] for s in steps], i32)
    kj_tab = jnp.asarray([s[1] for s in steps], i32)
    nh = DIFF_HEADS
    grid_spec = pltpu.PrefetchScalarGridSpec(
        num_scalar_prefetch=2,
        grid=(batch, nh, len(steps)),
        in_specs=[
            pl.BlockSpec((tq, 128), lambda b, h, s, qi, kj: (b * nq + qi[s], h)),
            pl.BlockSpec((tq, 128), lambda b, h, s, qi, kj: (b * nq + kj[s], nh + h)),
            pl.BlockSpec((tq, 128), lambda b, h, s, qi, kj: (b * nq + kj[s], 2 * nh + h)),
            pl.BlockSpec((1, tq, tq), lambda b, h, s, qi, kj: (h, 0, 0)),
            pl.BlockSpec((1, tq, tq), lambda b, h, s, qi, kj: (h, 0, 0)),
            pl.BlockSpec((2 * DIFF_DH, 1), lambda b, h, s, qi, kj: (0, 0)),
            pl.BlockSpec(memory_space=pltpu.SMEM),
        ],
        out_specs=pl.BlockSpec((tq, 128), lambda b, h, s, qi, kj: (b * nq + qi[s], h)),
        scratch_shapes=[pltpu.VMEM((2 * tq, 128), bf16),
                        pltpu.VMEM((1, 2 * tq), f32),
                        pltpu.VMEM((1, 2 * tq), f32),
                        pltpu.VMEM((2 * DIFF_DH, 2 * tq), f32)],
    )
    return pl.pallas_call(
        functools.partial(_diff_attn_kernel, tq=tq, lam_init=lam_init),
        grid_spec=grid_spec,
        out_shape=jax.ShapeDtypeStruct((batch * seq, DIFF_W), bf16),
        compiler_params=pltpu.CompilerParams(dimension_semantics=("arbitrary", "arbitrary", "arbitrary"),
                                             vmem_limit_bytes=VMEM_LIMIT),
        name="diff_attn",
    )(qi_tab, kj_tab, dqkv, dqkv, dqkv, bias0, bias1, norm_w_col, lam)


def _t5_bucket(dist):
    max_exact = REL_BUCKETS // 2
    d = jnp.maximum(dist, 0)
    large = max_exact + (jnp.log(jnp.maximum(d, 1).astype(f32) / max_exact)
                         / math.log(REL_MAX_DIST / max_exact) * (REL_BUCKETS - max_exact)).astype(i32)
    large = jnp.minimum(large, REL_BUCKETS - 1)
    return jnp.where(d < max_exact, d, large)


def _attn_bias_tiles(rel_bias, tq):
    assert tq >= REL_MAX_DIST
    table = rel_bias[_t5_bucket(jnp.arange(2 * tq))].astype(f32)
    far = rel_bias[REL_BUCKETS - 1].astype(f32)
    table = table - far[None, :]
    jk = jnp.arange(tq)[:, None]
    iq = jnp.arange(tq)[None, :]
    rel0 = iq - jk
    b0 = jnp.where((rel0 >= 0)[None], jnp.transpose(table[jnp.clip(rel0, 0, 2 * tq - 1)], (2, 0, 1)), NEG_BIG)
    b1 = jnp.transpose(table[tq + rel0], (2, 0, 1))
    return b0, b1


def _layer_norm(y, g, b):
    mu = jnp.mean(y, axis=-1, keepdims=True)
    yc = y - mu
    var = jnp.mean(yc * yc, axis=-1, keepdims=True)
    return yc * lax.rsqrt(var + LN_EPS) * g + b


def _out_proj_kernel(og_ref, od_ref, x_ref, w1_ref, w2_ref, g_ref, b_ref, rwt_ref, rb_ref, x1_ref, lt_ref, *, alpha):
    y = jnp.dot(og_ref[...], w1_ref[...], preferred_element_type=f32)
    y = y + jnp.dot(od_ref[...], w2_ref[...], preferred_element_type=f32)
    x1 = _layer_norm(alpha * x_ref[...] + y, g_ref[...], b_ref[...])
    x1_ref[...] = x1
    lt_ref[...] = lax.dot_general(rwt_ref[...], x1, (((1,), (1,)), ((), ())), preferred_element_type=f32,
                                  precision=lax.Precision.HIGHEST) + rb_ref[...]


def _out_proj(o_gdn, o_diff, x2d, w1, w2, g, b, rwt, rb_col, alpha):
    n, d = x2d.shape
    tm = _tile(n, 512)
    const = lambda i: (0, 0)
    return pl.pallas_call(
        functools.partial(_out_proj_kernel, alpha=alpha),
        grid=(n // tm,),
        in_specs=[pl.BlockSpec((tm, GDN_W), lambda i: (i, 0)),
                  pl.BlockSpec((tm, DIFF_W), lambda i: (i, 0)),
                  pl.BlockSpec((tm, d), lambda i: (i, 0)),
                  pl.BlockSpec((GDN_W, d), const),
                  pl.BlockSpec((DIFF_W, d), const),
                  pl.BlockSpec((1, d), const),
                  pl.BlockSpec((1, d), const),
                  pl.BlockSpec((N_EXPERTS, d), const),
                  pl.BlockSpec((N_EXPERTS, 1), const)],
        out_specs=[pl.BlockSpec((tm, d), lambda i: (i, 0)),
                   pl.BlockSpec((N_EXPERTS, tm), lambda i: (0, i))],
        out_shape=[jax.ShapeDtypeStruct((n, d), f32),
                   jax.ShapeDtypeStruct((N_EXPERTS, n), f32)],
        compiler_params=pltpu.CompilerParams(dimension_semantics=("arbitrary",),
                                             vmem_limit_bytes=VMEM_LIMIT),
        name="out_proj_ln1_router",
    )(o_gdn, o_diff, x2d, w1, w2, g, b, rwt, rb_col)


def _route_kernel(lt_ref, idx_ref, gate_ref, dest_ref, pend_ref, carry_ref, pstart_ref, *, tn, bm):
    phase = pl.program_id(0)
    j = pl.program_id(1)
    nt = pl.num_programs(1)
    ne = N_EXPERTS

    @pl.when(j == 0)
    def _():
        carry_ref[...] = jnp.zeros_like(carry_ref)

    logits = lt_ref[...]
    eio = lax.broadcasted_iota(i32, (ne, tn), 0)
    vals, idxs = [], []
    for _ in range(TOP_K):
        m = jnp.max(logits, axis=0, keepdims=True)
        sel = jnp.min(jnp.where(logits == m, eio, ne), axis=0, keepdims=True)
        vals.append(m)
        idxs.append(sel)
        logits = jnp.where(eio == sel, -jnp.inf, logits)
    onehot = jnp.zeros((ne, tn), f32)
    for sel in idxs:
        onehot = onehot + (eio == sel).astype(f32)

    @pl.when(phase == 1)
    def _():
        ex = [jnp.exp(v - vals[0]) for v in vals]
        denom = ex[0] + ex[1] + ex[2] + ex[3]
        su = (lax.broadcasted_iota(i32, (tn, tn), 0) < lax.broadcasted_iota(i32, (tn, tn), 1)).astype(bf16)
        rank = jnp.dot(onehot.astype(bf16), su, preferred_element_type=f32) + carry_ref[...] + pstart_ref[...]
        for k in range(TOP_K):
            idx_ref[k:k + 1, :] = idxs[k]
            gate_ref[k:k + 1, :] = ex[k] / denom
            dest_ref[k:k + 1, :] = jnp.sum(jnp.where(eio == idxs[k], rank, 0.0), axis=0, keepdims=True).astype(i32)

    carry_ref[...] = carry_ref[...] + jnp.sum(onehot, axis=1, keepdims=True)

    @pl.when((phase == 0) & (j == nt - 1))
    def _():
        counts = carry_ref[...]
        padded = jnp.floor((counts + (bm - 1)) / bm) * bm
        tri = (lax.broadcasted_iota(i32, (ne, ne), 0) >= lax.broadcasted_iota(i32, (ne, ne), 1)).astype(f32)
        pend = jnp.dot(tri, jnp.broadcast_to(padded, (ne, 128)), preferred_element_type=f32,
                       precision=lax.Precision.HIGHEST)
        pend_ref[...] = pend.astype(i32)
        pstart_ref[...] = pend[:, 0:1] - padded


def _route(logits_t, bm):
    ne, n = logits_t.shape
    tn = _tile(n, 512)
    nt = n // tn
    tok_map = lambda p, j: (0, j * p)
    return pl.pallas_call(
        functools.partial(_route_kernel, tn=tn, bm=bm),
        grid=(2, nt),
        in_specs=[pl.BlockSpec((ne, tn), lambda p, j: (0, j))],
        out_specs=[pl.BlockSpec((TOP_K, tn), tok_map),
                   pl.BlockSpec((TOP_K, tn), tok_map),
                   pl.BlockSpec((TOP_K, tn), tok_map),
                   pl.BlockSpec((ne, 128), lambda p, j: (0, 0))],
        out_shape=[jax.ShapeDtypeStruct((TOP_K, n), i32),
                   jax.ShapeDtypeStruct((TOP_K, n), f32),
                   jax.ShapeDtypeStruct((TOP_K, n), i32),
                   jax.ShapeDtypeStruct((ne, 128), i32)],
        scratch_shapes=[pltpu.VMEM((ne, 1), f32), pltpu.VMEM((ne, 1), f32)],
        compiler_params=pltpu.CompilerParams(dimension_semantics=("arbitrary", "arbitrary"),
                                             vmem_limit_bytes=VMEM_LIMIT),
        name="route",
    )(logits_t)


def _dispatch_kernel(dest_ref, x_hbm, xs_in_hbm, xs_hbm, sem, *, tt):
    del xs_in_hbm
    base = pl.program_id(0) * tt

    def row_copy(t, k):
        return pltpu.make_async_copy(x_hbm.at[pl.ds(base + t, 1)], xs_hbm.at[pl.ds(dest_ref[k, t], 1)], sem)

    def issue(t, carry):
        for k in range(TOP_K):
            row_copy(t, k).start()
        return carry

    def drain(t, carry):
        for k in range(TOP_K):
            row_copy(t, k).wait()
        return carry

    lax.fori_loop(0, tt, issue, 0)
    lax.fori_loop(0, tt, drain, 0)


def _dispatch(dest, x1, n_slots):
    n, d = x1.shape
    tt = _tile(n, 512)
    xs0 = jnp.zeros((n_slots, d), f32)
    return pl.pallas_call(
        functools.partial(_dispatch_kernel, tt=tt),
        grid=(n // tt,),
        in_specs=[pl.BlockSpec((TOP_K, tt), lambda i: (0, i), memory_space=pltpu.SMEM),
                  pl.BlockSpec(memory_space=pl.ANY),
                  pl.BlockSpec(memory_space=pl.ANY)],
        out_specs=pl.BlockSpec(memory_space=pl.ANY),
        out_shape=jax.ShapeDtypeStruct((n_slots, d), f32),
        scratch_shapes=[pltpu.SemaphoreType.DMA(())],
        input_output_aliases={2: 0},
        compiler_params=pltpu.CompilerParams(dimension_semantics=("arbitrary",), has_side_effects=True),
        name="dispatch",
    )(dest, x1, xs0)


def _expert_kernel(be_ref, nu_ref, xs_ref, wgu_ref, bgu_ref, wd_ref, bd_ref, ys_ref, wgu_c, wd_c):
    i = pl.program_id(0)

    @pl.when(i < nu_ref[0])
    def _():
        e = be_ref[i]
        prev = be_ref[jnp.maximum(i - 1, 0)]

        @pl.when((i == 0) | (e != prev))
        def _():
            wgu_c[...] = wgu_ref[0].astype(bf16)
            wd_c[...] = wd_ref[0].astype(bf16)

        x = xs_ref[...].astype(bf16)
        gu = jnp.dot(x, wgu_c[...], preferred_element_type=f32) + bgu_ref[0]
        gate = jnp.minimum(gu[:, :D_EXPERT], SWIGLU_LIMIT)
        up = jnp.clip(gu[:, D_EXPERT:], -SWIGLU_LIMIT, SWIGLU_LIMIT)
        act = (up + 1.0) * gate * jax.nn.sigmoid(SWIGLU_ALPHA * gate)
        ys_ref[...] = jnp.dot(act.astype(bf16), wd_c[...], preferred_element_type=f32) + bd_ref[0]

    @pl.when(i >= nu_ref[0])
    def _():
        ys_ref[...] = jnp.zeros_like(ys_ref)


def _expert_ffn(blk_expert, n_used, xs, w_gate_up, b_gate_up, w_down, b_down, bm):
    n_slots, d = xs.shape
    ne = w_gate_up.shape[0]
    n_blocks = n_slots // bm
    grid_spec = pltpu.PrefetchScalarGridSpec(
        num_scalar_prefetch=2,
        grid=(n_blocks,),
        in_specs=[pl.BlockSpec((bm, d), lambda i, be, nu: (i, 0)),
                  pl.BlockSpec((1, d, 2 * D_EXPERT), lambda i, be, nu: (be[i], 0, 0)),
                  pl.BlockSpec((1, 1, 2 * D_EXPERT), lambda i, be, nu: (be[i], 0, 0)),
                  pl.BlockSpec((1, D_EXPERT, d), lambda i, be, nu: (be[i], 0, 0)),
                  pl.BlockSpec((1, 1, d), lambda i, be, nu: (be[i], 0, 0))],
        out_specs=pl.BlockSpec((bm, d), lambda i, be, nu: (i, 0)),
        scratch_shapes=[pltpu.VMEM((d, 2 * D_EXPERT), bf16), pltpu.VMEM((D_EXPERT, d), bf16)],
    )
    return pl.pallas_call(
        _expert_kernel,
        grid_spec=grid_spec,
        out_shape=jax.ShapeDtypeStruct((n_slots, d), f32),
        compiler_params=pltpu.CompilerParams(dimension_semantics=("arbitrary",),
                                             vmem_limit_bytes=VMEM_LIMIT),
        name="expert_ffn",
    )(blk_expert, n_used, xs, w_gate_up, b_gate_up.reshape(ne, 1, 2 * D_EXPERT), w_down,
      b_down.reshape(ne, 1, d))


def _combine_kernel(dest_ref, ys_hbm, gate_ref, x1_ref, p_ref, g2_ref, b2_ref, wg_ref, bg_ref, wp_ref,
                    g3_ref, b3_ref, o_ref, buf_ref, sem, *, tt, alpha):
    def row_copy(t, k):
        return pltpu.make_async_copy(ys_hbm.at[pl.ds(dest_ref[k, t], 1)], buf_ref.at[k, pl.ds(t, 1)], sem)

    def issue(t, carry):
        for k in range(TOP_K):
            row_copy(t, k).start()
        return carry

    def drain(t, carry):
        for k in range(TOP_K):
            row_copy(t, k).wait()
        return carry

    lax.fori_loop(0, tt, issue, 0)
    lax.fori_loop(0, tt, drain, 0)

    gates = gate_ref[...]
    ffn = buf_ref[0] * gates[:, 0:1]
    for k in range(1, TOP_K):
        ffn = ffn + buf_ref[k] * gates[:, k:k + 1]
    x2 = _layer_norm(alpha * x1_ref[...] + ffn, g2_ref[...], b2_ref[...])
    gate = jax.nn.sigmoid(jnp.dot(x2.astype(bf16), wg_ref[...], preferred_element_type=f32) + bg_ref[...])
    emb = jnp.dot(p_ref[...].astype(bf16), wp_ref[...], preferred_element_type=f32)
    o_ref[...] = _layer_norm(alpha * x2 + gate * emb, g3_ref[...], b3_ref[...])


def _combine(dest, ys, gates_tok, x1, p2d, g2, b2, wg, bg, wp, g3, b3, alpha):
    n, d = x1.shape
    tt = _tile(n, 256)
    pd = p2d.shape[1]
    const = lambda i: (0, 0)
    return pl.pallas_call(
        functools.partial(_combine_kernel, tt=tt, alpha=alpha),
        grid=(n // tt,),
        in_specs=[pl.BlockSpec((TOP_K, tt), lambda i: (0, i), memory_space=pltpu.SMEM),
                  pl.BlockSpec(memory_space=pl.ANY),
                  pl.BlockSpec((tt, TOP_K), lambda i: (i, 0)),
                  pl.BlockSpec((tt, d), lambda i: (i, 0)),
                  pl.BlockSpec((tt, pd), lambda i: (i, 0)),
                  pl.BlockSpec((1, d), const),
                  pl.BlockSpec((1, d), const),
                  pl.BlockSpec((d, d), const),
                  pl.BlockSpec((1, d), const),
                  pl.BlockSpec((pd, d), const),
                  pl.BlockSpec((1, d), const),
                  pl.BlockSpec((1, d), const)],
        out_specs=pl.BlockSpec((tt, d), lambda i: (i, 0)),
        out_shape=jax.ShapeDtypeStruct((n, d), f32),
        scratch_shapes=[pltpu.VMEM((TOP_K, tt, d), f32), pltpu.SemaphoreType.DMA(())],
        compiler_params=pltpu.CompilerParams(dimension_semantics=("arbitrary",),
                                             vmem_limit_bytes=VMEM_LIMIT),
        name="combine_ln2_ple_ln3",
    )(dest, ys, gates_tok, x1, p2d, g2, b2, wg, bg, wp, g3, b3)


def _permute_in_proj(w):
    d = w.shape[0]
    o = 0
    gqkv = w[:, o:o + 3 * GDN_W]; o += 3 * GDN_W
    ga_gb = w[:, o:o + 2 * GDN_HEADS]; o += 2 * GDN_HEADS
    gz = w[:, o:o + GDN_W]; o += GDN_W
    dqkv = w[:, o:o + 3 * DIFF_W]
    pad = jnp.zeros((d, AB_W - 2 * GDN_HEADS), w.dtype)
    return jnp.concatenate([gqkv, gz, dqkv, ga_gb, pad], axis=1).astype(bf16)


def _lane_row(v, width):
    return jnp.zeros((1, width), f32).at[0, :v.shape[0]].set(v.astype(f32))


def kernel(x, p, w_in, conv_w, gdn_a_log, gdn_dt_bias, gdn_norm_w, diff_lq1, diff_lk1, diff_lq2, diff_lk2,
           diff_norm_w, w_out, rel_bias, ln1_g, ln1_b, router_w, router_b, w_gate_up, b_gate_up, w_down, b_down,
           ln2_g, ln2_b, ple_gate_w, ple_gate_b, ple_proj_w, ln3_g, ln3_b):
    batch, seq, d = x.shape
    depth = w_in.shape[0]
    n = batch * seq
    alpha = (2 * depth) ** 0.25
    bm = 512 if n * TOP_K >= 512 * N_EXPERTS else 128
    n_blocks = -(-n * TOP_K // bm) + N_EXPERTS
    tq = _tile(seq, 512)
    bias0, bias1 = _attn_bias_tiles(rel_bias, tq)
    row = lambda v: v.astype(f32).reshape(1, -1)

    x2d = x.reshape(n, d)
    for i in range(depth):
        lam_init = 0.8 - 0.6 * math.exp(-0.3 * i)
        gqkv, gz, dqkv, ab = _in_proj(x2d, _permute_in_proj(w_in[i]))
        o_gdn = _gdn(gqkv, ab, gz, conv_w[i].astype(f32), _lane_row(jnp.exp(gdn_a_log[i].astype(f32)), AB_W),
                     _lane_row(gdn_dt_bias[i], AB_W), row(gdn_norm_w[i]), batch, seq)
        lam = (jnp.exp(jnp.sum(diff_lq1[i].astype(f32) * diff_lk1[i].astype(f32)))
               - jnp.exp(jnp.sum(diff_lq2[i].astype(f32) * diff_lk2[i].astype(f32))) + lam_init)
        o_diff = _diff_attn(dqkv, bias0, bias1, diff_norm_w[i].astype(f32).reshape(-1, 1),
                            lam.reshape(1, 1).astype(f32), batch, seq, lam_init)
        w_o = w_out[i].astype(bf16)
        x1, logits_t = _out_proj(o_gdn, o_diff, x2d, w_o[:GDN_W], w_o[GDN_W:], row(ln1_g[i]), row(ln1_b[i]),
                                 jnp.transpose(router_w[i]).astype(f32), router_b[i].astype(f32).reshape(-1, 1), alpha)
        _, gates, dest, pend = _route(logits_t, bm)
        pend = pend[:, 0]
        blk_expert = jnp.minimum(
            jnp.sum((pend[None, :] <= (jnp.arange(n_blocks, dtype=i32) * bm)[:, None]).astype(i32), axis=1),
            N_EXPERTS - 1).astype(i32)
        n_used = (pend[N_EXPERTS - 1:] // bm).astype(i32)
        xs = _dispatch(dest, x1, n_blocks * bm)
        ys = _expert_ffn(blk_expert, n_used, xs, w_gate_up[i], b_gate_up[i], w_down[i], b_down[i], bm)
        x2d = _combine(dest, ys, jnp.transpose(gates), x1, p[i].reshape(n, -1), row(ln2_g[i]), row(ln2_b[i]),
                       ple_gate_w[i].astype(bf16), row(ple_gate_b[i]), ple_proj_w[i].astype(bf16),
                       row(ln3_g[i]), row(ln3_b[i]), alpha)
    return x2d.reshape(batch, seq, d)
```

```python
import functools
import math

import jax
import jax.numpy as jnp
from jax import lax
from jax.experimental import pallas as pl
from jax.experimental.pallas import tpu as pltpu

f32 = jnp.float32
bf16 = jnp.bfloat16
i32 = jnp.int32

GDN_HEADS = 4
GDN_DK = 128
GDN_DV = 128
GDN_CONV = 4
GDN_CHUNK = 64
DIFF_HEADS = 4
DIFF_DH = 64
REL_BUCKETS = 32
REL_MAX_DIST = 128
N_EXPERTS = 32
TOP_K = 4
D_EXPERT = 1024
SWIGLU_LIMIT = 7.0
SWIGLU_ALPHA = 1.702
LN_EPS = 1e-5
RMS_EPS = 1e-6

GDN_W = GDN_HEADS * GDN_DK
DIFF_W = DIFF_HEADS * 2 * DIFF_DH
AB_W = 128
PROJ_W = 3 * GDN_W + GDN_W + 3 * DIFF_W + AB_W

VMEM_LIMIT = 56 * 1024 * 1024
NEG_BIG = -1e30
LOG2E = math.log2(math.e)
GDN_TILE = 256
ATTN_TILE = 512
ATTN_STRIP = 256


def _tile(n, want):
    t = min(n, want)
    assert n % t == 0, (n, t)
    return t


def _in_proj_kernel(x_ref, w_ref, gqkv_ref, z_ref, dqkv_ref, ab_ref):
    xb = x_ref[...].astype(bf16)

    def mm(c0, width):
        return jnp.dot(xb, w_ref[:, c0:c0 + width], preferred_element_type=f32)

    for j in range(3):
        gqkv_ref[:, j * GDN_W:(j + 1) * GDN_W] = mm(j * GDN_W, GDN_W).astype(bf16)
    z_ref[...] = mm(3 * GDN_W, GDN_W).astype(bf16)
    base = 4 * GDN_W
    for j in range(3):
        dqkv_ref[:, j * DIFF_W:(j + 1) * DIFF_W] = mm(base + j * DIFF_W, DIFF_W).astype(bf16)
    ab_ref[...] = mm(base + 3 * DIFF_W, AB_W)


def _in_proj(x2d, w_perm):
    n, d = x2d.shape
    tm = _tile(n, 512)
    return pl.pallas_call(
        _in_proj_kernel,
        grid=(n // tm,),
        in_specs=[pl.BlockSpec((tm, d), lambda i: (i, 0)),
                  pl.BlockSpec((d, PROJ_W), lambda i: (0, 0))],
        out_specs=[pl.BlockSpec((tm, 3 * GDN_W), lambda i: (i, 0)),
                   pl.BlockSpec((tm, GDN_W), lambda i: (i, 0)),
                   pl.BlockSpec((tm, 3 * DIFF_W), lambda i: (i, 0)),
                   pl.BlockSpec((tm, AB_W), lambda i: (i, 0))],
        out_shape=[jax.ShapeDtypeStruct((n, 3 * GDN_W), bf16),
                   jax.ShapeDtypeStruct((n, GDN_W), bf16),
                   jax.ShapeDtypeStruct((n, 3 * DIFF_W), bf16),
                   jax.ShapeDtypeStruct((n, AB_W), f32)],
        compiler_params=pltpu.CompilerParams(dimension_semantics=("arbitrary",),
                                             vmem_limit_bytes=VMEM_LIMIT),
        name="in_proj",
    )(x2d, w_perm)


def _mm(a, b):
    return jnp.dot(a.astype(bf16), b.astype(bf16), preferred_element_type=f32)


def _mm_nt(a, b):
    return lax.dot_general(a.astype(bf16), b.astype(bf16), (((1,), (1,)), ((), ())), preferred_element_type=f32)


def _mm_tn(a, b):
    return lax.dot_general(a.astype(bf16), b.astype(bf16), (((0,), (0,)), ((), ())), preferred_element_type=f32)


def _gdn_kernel(qkv_ref, ab_ref, z_ref, convw_ref, aexp_ref, dtb_ref, nw_ref, o_ref, xe_ref, state_ref, *, tb):
    c = GDN_CHUNK
    t = pl.program_id(1)

    @pl.when(t == 0)
    def _():
        xe_ref[0:8, :] = jnp.zeros((8, 3 * GDN_W), f32)
        state_ref[...] = jnp.zeros_like(state_ref)

    xe_ref[8:8 + tb, :] = qkv_ref[...].astype(f32)

    def conv_silu(c0):
        acc = xe_ref[5:5 + tb, c0:c0 + 128] * convw_ref[0:1, c0:c0 + 128]
        for j in range(1, GDN_CONV):
            acc = acc + xe_ref[5 + j:5 + j + tb, c0:c0 + 128] * convw_ref[j:j + 1, c0:c0 + 128]
        return acc * jax.nn.sigmoid(acc)

    def l2n(v):
        return v * lax.rsqrt(jnp.sum(v * v, axis=-1, keepdims=True) + 1e-6)

    ab = ab_ref[...]
    xg = ab + dtb_ref[...]
    softplus = jnp.maximum(xg, 0.0) + jnp.log(1.0 + jnp.exp(-jnp.abs(xg)))
    glog = -aexp_ref[...] * softplus
    beta = jax.nn.sigmoid(ab)

    row = lax.broadcasted_iota(i32, (c, c), 0)
    col = lax.broadcasted_iota(i32, (c, c), 1)
    lower = row >= col
    strict = row > col
    eye = (row == col).astype(f32)
    nw = nw_ref[...]

    brow = lax.broadcasted_iota(i32, (tb, tb), 0)
    bcol = lax.broadcasted_iota(i32, (tb, tb), 1)
    chunk_lower = ((brow // c == bcol // c) & (brow >= bcol)).astype(f32)
    gcum_all = jnp.dot(chunk_lower, glog, preferred_element_type=f32, precision=lax.Precision.HIGHEST)
    gcum_t = jnp.transpose(gcum_all)

    heads = range(GDN_HEADS)
    qh = [l2n(conv_silu(h * GDN_DK)) * (GDN_DK ** -0.5) for h in heads]
    kh = [l2n(conv_silu(GDN_W + h * GDN_DK)) for h in heads]
    vh = [conv_silu(2 * GDN_W + h * GDN_DV) for h in heads]

    pairs = [(ci, h) for ci in range(tb // c) for h in heads]
    rows = lambda ci: slice(ci * c, (ci + 1) * c)
    g_col = [gcum_all[rows(ci), h:h + 1] for ci, h in pairs]
    g_row = [gcum_t[h:h + 1, rows(ci)] for ci, h in pairs]
    g_last = [gcum_all[ci * c + c - 1:ci * c + c, h:h + 1] for ci, h in pairs]
    b_col = [beta[rows(ci), GDN_HEADS + h:GDN_HEADS + h + 1] for ci, h in pairs]
    decay = [jnp.where(lower, jnp.exp(jnp.where(lower, gc - gr, 0.0)), 0.0) for gc, gr in zip(g_col, g_row)]
    eg = [jnp.exp(gc) for gc in g_col]
    q = [qh[h][rows(ci)] for ci, h in pairs]
    k = [kh[h][rows(ci)] for ci, h in pairs]
    kb = [kk * bc for kk, bc in zip(k, b_col)]
    rhs = [jnp.concatenate([vh[h][rows(ci)] * bc, kbi * e], axis=1)
           for (ci, h), bc, kbi, e in zip(pairs, b_col, kb, eg)]
    kkqk = [_mm_nt(jnp.concatenate([kbi, qi], axis=0), ki) for kbi, qi, ki in zip(kb, q, k)]
    lmat = [jnp.where(strict, x[:c] * d, 0.0) for x, d in zip(kkqk, decay)]
    intra = [jnp.where(lower, x[c:] * d, 0.0) for x, d in zip(kkqk, decay)]
    a = [eye - l for l in lmat]
    p = [_mm(l, l) for l in lmat]
    for _ in range(int(math.log2(c)) - 2):
        r = [_mm(pi, jnp.concatenate([ai, pi], axis=1)) for ai, pi in zip(a, p)]
        a = [ai + ri[:, :c] for ai, ri in zip(a, r)]
        p = [ri[:, c:] for ri in r]
    tinv = [ai + _mm(pi, ai) for ai, pi in zip(a, p)]
    sol = [_mm(ti, ri) for ti, ri in zip(tinv, rhs)]
    q_dec = [qi * e for qi, e in zip(q, eg)]
    k_dec = [ki * jnp.exp(gl - gc) for ki, gl, gc in zip(k, g_last, g_col)]
    wq = [jnp.concatenate([s[:, GDN_DV:], qd], axis=0) for s, qd in zip(sol, q_dec)]

    state = [state_ref[h] for h in heads]
    for ci in range(tb // c):
        ids = [ci * GDN_HEADS + h for h in heads]
        r = [_mm(wq[i], state[h]) for i, h in zip(ids, heads)]
        v_new = [sol[i][:, :GDN_DV] - ri[:c] for i, ri in zip(ids, r)]
        o = [ri[c:] + _mm(intra[i], vn) for i, ri, vn in zip(ids, r, v_new)]
        state = [state[h] * jnp.exp(g_last[i]) + _mm_tn(k_dec[i], vn) for i, h, vn in zip(ids, heads, v_new)]
        for h in heads:
            zg = z_ref[rows(ci), h * GDN_DV:(h + 1) * GDN_DV].astype(f32)
            on = o[h] * lax.rsqrt(jnp.mean(o[h] * o[h], axis=-1, keepdims=True) + RMS_EPS) * nw
            o_ref[rows(ci), h * GDN_DV:(h + 1) * GDN_DV] = (on * (zg * jax.nn.sigmoid(zg))).astype(bf16)
    for h in heads:
        state_ref[h] = state[h]

    xe_ref[0:8, :] = xe_ref[tb:tb + 8, :]


def _gdn(gqkv, ab, z, conv_w, a_exp, dt_bias, norm_w, batch, seq):
    tb = _tile(seq, GDN_TILE)
    nt = seq // tb
    row_map = lambda b, t: (b * nt + t, 0)
    const = lambda b, t: (0, 0)
    return pl.pallas_call(
        functools.partial(_gdn_kernel, tb=tb),
        grid=(batch, nt),
        in_specs=[pl.BlockSpec((tb, 3 * GDN_W), row_map),
                  pl.BlockSpec((tb, AB_W), row_map),
                  pl.BlockSpec((tb, GDN_W), row_map),
                  pl.BlockSpec((GDN_CONV, 3 * GDN_W), const),
                  pl.BlockSpec((1, AB_W), const),
                  pl.BlockSpec((1, AB_W), const),
                  pl.BlockSpec((1, GDN_DV), const)],
        out_specs=pl.BlockSpec((tb, GDN_W), row_map),
        out_shape=jax.ShapeDtypeStruct((batch * seq, GDN_W), bf16),
        scratch_shapes=[pltpu.VMEM((tb + 8, 3 * GDN_W), f32),
                        pltpu.VMEM((GDN_HEADS, GDN_DK, GDN_DV), f32)],
        compiler_params=pltpu.CompilerParams(dimension_semantics=("arbitrary", "arbitrary"),
                                             vmem_limit_bytes=VMEM_LIMIT),
        name="gdn",
    )(gqkv, ab, z, conv_w, a_exp, dt_bias, norm_w)


def _diff_attn_kernel(qi_ref, kj_ref, q_ref, k_ref, v_ref, bias0_ref, bias1_ref, nw_ref, lam_ref, o_ref,
                      qt_ref, m_ref, l_ref, alpha_ref, acc_ref, s_ref, p_ref, *, tq, lam_init):
    step = pl.program_id(2)
    qi = qi_ref[step]
    kj = kj_ref[step]

    @pl.when(kj == 0)
    def _():
        qt = jnp.transpose(q_ref[...].astype(f32)).astype(bf16)
        first = lax.broadcasted_iota(i32, qt.shape, 0) < DIFF_DH
        zero = jnp.zeros_like(qt)
        qt_ref[:, 0:tq] = jnp.where(first, qt, zero)
        qt_ref[:, tq:2 * tq] = jnp.where(first, zero, qt)
        m_ref[...] = jnp.full_like(m_ref, NEG_BIG)
        l_ref[...] = jnp.zeros_like(l_ref)
        acc_ref[...] = jnp.zeros_like(acc_ref)

    def update(bias_ref):
        vt = jnp.transpose(v_ref[...].astype(f32)).astype(bf16)
        n_strips = 2 * tq // ATTN_STRIP

        def qk(c):
            sl = slice(c * ATTN_STRIP, (c + 1) * ATTN_STRIP)
            s_ref[:, sl] = jnp.dot(k_ref[...], qt_ref[:, sl], preferred_element_type=f32)

        qk(0)
        for c in range(n_strips):
            sl = slice(c * ATTN_STRIP, (c + 1) * ATTN_STRIP)
            if c + 1 < n_strips:
                qk(c + 1)
            for half in range(ATTN_STRIP // 128):
                lo = c * ATTN_STRIP + half * 128
                cs = slice(lo, lo + 128)

                def scores():
                    if bias_ref is None:
                        return s_ref[:, cs]
                    return s_ref[:, cs] + bias_ref[0, :, lo % tq:lo % tq + 128]

                m_prev = m_ref[:, cs]
                m_new = jnp.maximum(m_prev, jnp.max(scores(), axis=0, keepdims=True))
                alpha = jnp.exp2(m_prev - m_new)
                p = jnp.exp2(scores() - m_new)
                l_ref[:, cs] = alpha * l_ref[:, cs] + jnp.sum(p, axis=0, keepdims=True)
                m_ref[:, cs] = m_new
                alpha_ref[:, cs] = alpha
                p_ref[:, cs] = p.astype(bf16)
            pv = jnp.dot(vt, p_ref[:, sl], preferred_element_type=f32)
            acc_ref[:, sl] = alpha_ref[:, sl] * acc_ref[:, sl] + pv

    @pl.when(kj == qi)
    def _():
        update(bias0_ref)

    @pl.when(kj == qi - 1)
    def _():
        update(bias1_ref)

    @pl.when(kj < qi - 1)
    def _():
        update(None)

    @pl.when(kj == qi)
    def _():
        o = acc_ref[...] * (1.0 / l_ref[...])
        d = o[:, :tq] - lam_ref[0, 0] * o[:, tq:]
        ms = jnp.mean(d * d, axis=0, keepdims=True)
        y = d * lax.rsqrt(ms + RMS_EPS) * (nw_ref[...] * (1.0 - lam_init))
        o_ref[...] = jnp.transpose(y).astype(bf16)


def _diff_attn(dqkv, bias0, bias1, norm_w_col, lam, batch, seq, lam_init):
    tq = _tile(seq, ATTN_TILE)
    nq = seq // tq
    steps = [(qi, kj) for qi in range(nq) for kj in range(qi + 1)]
    qi_tab = jnp.asarray([s[0] for s in steps], i32)
    kj_tab = jnp.asarray([s[1] for s in steps], i32)
    nh = DIFF_HEADS
    grid_spec = pltpu.PrefetchScalarGridSpec(
        num_scalar_prefetch=2,
        grid=(batch, nh, len(steps)),
        in_specs=[
            pl.BlockSpec((tq, 128), lambda b, h, s, qi, kj: (b * nq + qi[s], h)),
            pl.BlockSpec((tq, 128), lambda b, h, s, qi, kj: (b * nq + kj[s], nh + h)),
            pl.BlockSpec((tq, 128), lambda b, h, s, qi, kj: (b * nq + kj[s], 2 * nh + h)),
            pl.BlockSpec((1, tq, tq), lambda b, h, s, qi, kj: (h, 0, 0)),
            pl.BlockSpec((1, tq, tq), lambda b, h, s, qi, kj: (h, 0, 0)),
            pl.BlockSpec((2 * DIFF_DH, 1), lambda b, h, s, qi, kj: (0, 0)),
            pl.BlockSpec(memory_space=pltpu.SMEM),
        ],
        out_specs=pl.BlockSpec((tq, 128), lambda b, h, s, qi, kj: (b * nq + qi[s], h)),
        scratch_shapes=[pltpu.VMEM((2 * DIFF_DH, 2 * tq), bf16),
                        pltpu.VMEM((1, 2 * tq), f32),
                        pltpu.VMEM((1, 2 * tq), f32),
                        pltpu.VMEM((1, 2 * tq), f32),
                        pltpu.VMEM((2 * DIFF_DH, 2 * tq), f32),
                        pltpu.VMEM((tq, 2 * tq), f32),
                        pltpu.VMEM((tq, 2 * tq), bf16)],
    )
    return pl.pallas_call(
        functools.partial(_diff_attn_kernel, tq=tq, lam_init=lam_init),
        grid_spec=grid_spec,
        out_shape=jax.ShapeDtypeStruct((batch * seq, DIFF_W), bf16),
        compiler_params=pltpu.CompilerParams(dimension_semantics=("arbitrary", "arbitrary", "arbitrary"),
                                             vmem_limit_bytes=VMEM_LIMIT),
        name="diff_attn",
    )(qi_tab, kj_tab, dqkv, dqkv, dqkv, bias0, bias1, norm_w_col, lam)


def _t5_bucket(dist):
    max_exact = REL_BUCKETS // 2
    d = jnp.maximum(dist, 0)
    large = max_exact + (jnp.log(jnp.maximum(d, 1).astype(f32) / max_exact)
                         / math.log(REL_MAX_DIST / max_exact) * (REL_BUCKETS - max_exact)).astype(i32)
    large = jnp.minimum(large, REL_BUCKETS - 1)
    return jnp.where(d < max_exact, d, large)


def _toeplitz(v, t):
    n = v.shape[0]
    return jnp.tile(v, t)[:t * (n - 1)].reshape(t, n - 1)[:, :t]


def _attn_bias_tiles(rel_bias, tq):
    assert tq >= REL_MAX_DIST
    table = rel_bias[_t5_bucket(jnp.arange(2 * tq))].astype(f32)
    table = (table - rel_bias[REL_BUCKETS - 1].astype(f32)[None, :]) * LOG2E
    table = jnp.transpose(table)
    masked = jnp.full((table.shape[0], tq), NEG_BIG, f32)
    v0 = jnp.concatenate([table[:, :tq], masked], axis=1)
    v1 = jnp.concatenate([table[:, tq:], table[:, :tq]], axis=1)
    toe = jax.vmap(functools.partial(_toeplitz, t=tq))
    return toe(v0), toe(v1)


def _layer_norm(y, g, b):
    mu = jnp.mean(y, axis=-1, keepdims=True)
    yc = y - mu
    var = jnp.mean(yc * yc, axis=-1, keepdims=True)
    return yc * lax.rsqrt(var + LN_EPS) * g + b


def _out_proj_kernel(og_ref, od_ref, x_ref, w1_ref, w2_ref, g_ref, b_ref, rwt_ref, rb_ref, x1_ref, lt_ref, *, alpha):
    y = jnp.dot(og_ref[...], w1_ref[...], preferred_element_type=f32)
    y = y + jnp.dot(od_ref[...], w2_ref[...], preferred_element_type=f32)
    x1 = _layer_norm(alpha * x_ref[...] + y, g_ref[...], b_ref[...])
    x1_ref[...] = x1
    lt_ref[...] = lax.dot_general(rwt_ref[...], x1, (((1,), (1,)), ((), ())), preferred_element_type=f32,
                                  precision=lax.Precision.HIGHEST) + rb_ref[...]


def _out_proj(o_gdn, o_diff, x2d, w1, w2, g, b, rwt, rb_col, alpha):
    n, d = x2d.shape
    tm = _tile(n, 512)
    const = lambda i: (0, 0)
    return pl.pallas_call(
        functools.partial(_out_proj_kernel, alpha=alpha),
        grid=(n // tm,),
        in_specs=[pl.BlockSpec((tm, GDN_W), lambda i: (i, 0)),
                  pl.BlockSpec((tm, DIFF_W), lambda i: (i, 0)),
                  pl.BlockSpec((tm, d), lambda i: (i, 0)),
                  pl.BlockSpec((GDN_W, d), const),
                  pl.BlockSpec((DIFF_W, d), const),
                  pl.BlockSpec((1, d), const),
                  pl.BlockSpec((1, d), const),
                  pl.BlockSpec((N_EXPERTS, d), const),
                  pl.BlockSpec((N_EXPERTS, 1), const)],
        out_specs=[pl.BlockSpec((tm, d), lambda i: (i, 0)),
                   pl.BlockSpec((N_EXPERTS, tm), lambda i: (0, i))],
        out_shape=[jax.ShapeDtypeStruct((n, d), f32),
                   jax.ShapeDtypeStruct((N_EXPERTS, n), f32)],
        compiler_params=pltpu.CompilerParams(dimension_semantics=("arbitrary",),
                                             vmem_limit_bytes=VMEM_LIMIT),
        name="out_proj_ln1_router",
    )(o_gdn, o_diff, x2d, w1, w2, g, b, rwt, rb_col)


def _route_kernel(lt_ref, idx_ref, gate_ref, dest_ref, pend_ref, carry_ref, pstart_ref, *, tn, bm):
    phase = pl.program_id(0)
    j = pl.program_id(1)
    nt = pl.num_programs(1)
    ne = N_EXPERTS

    @pl.when(j == 0)
    def _():
        carry_ref[...] = jnp.zeros_like(carry_ref)

    logits = lt_ref[...]
    eio = lax.broadcasted_iota(i32, (ne, tn), 0)
    vals, idxs = [], []
    for _ in range(TOP_K):
        m = jnp.max(logits, axis=0, keepdims=True)
        sel = jnp.min(jnp.where(logits == m, eio, ne), axis=0, keepdims=True)
        vals.append(m)
        idxs.append(sel)
        logits = jnp.where(eio == sel, -jnp.inf, logits)
    onehot = jnp.zeros((ne, tn), f32)
    for sel in idxs:
        onehot = onehot + (eio == sel).astype(f32)

    @pl.when(phase == 1)
    def _():
        ex = [jnp.exp(v - vals[0]) for v in vals]
        denom = ex[0] + ex[1] + ex[2] + ex[3]
        su = (lax.broadcasted_iota(i32, (tn, tn), 0) < lax.broadcasted_iota(i32, (tn, tn), 1)).astype(bf16)
        rank = jnp.dot(onehot.astype(bf16), su, preferred_element_type=f32) + carry_ref[...] + pstart_ref[...]
        for k in range(TOP_K):
            idx_ref[k:k + 1, :] = idxs[k]
            gate_ref[k:k + 1, :] = ex[k] / denom
            dest_ref[k:k + 1, :] = jnp.sum(jnp.where(eio == idxs[k], rank, 0.0), axis=0, keepdims=True).astype(i32)

    carry_ref[...] = carry_ref[...] + jnp.sum(onehot, axis=1, keepdims=True)

    @pl.when((phase == 0) & (j == nt - 1))
    def _():
        counts = carry_ref[...]
        padded = jnp.floor((counts + (bm - 1)) / bm) * bm
        tri = (lax.broadcasted_iota(i32, (ne, ne), 0) >= lax.broadcasted_iota(i32, (ne, ne), 1)).astype(f32)
        pend = jnp.dot(tri, jnp.broadcast_to(padded, (ne, 128)), preferred_element_type=f32,
                       precision=lax.Precision.HIGHEST)
        pend_ref[...] = pend.astype(i32)
        pstart_ref[...] = pend[:, 0:1] - padded


def _route(logits_t, bm):
    ne, n = logits_t.shape
    tn = _tile(n, 512)
    nt = n // tn
    tok_map = lambda p, j: (0, j * p)
    return pl.pallas_call(
        functools.partial(_route_kernel, tn=tn, bm=bm),
        grid=(2, nt),
        in_specs=[pl.BlockSpec((ne, tn), lambda p, j: (0, j))],
        out_specs=[pl.BlockSpec((TOP_K, tn), tok_map),
                   pl.BlockSpec((TOP_K, tn), tok_map),
                   pl.BlockSpec((TOP_K, tn), tok_map),
                   pl.BlockSpec((ne, 128), lambda p, j: (0, 0))],
        out_shape=[jax.ShapeDtypeStruct((TOP_K, n), i32),
                   jax.ShapeDtypeStruct((TOP_K, n), f32),
                   jax.ShapeDtypeStruct((TOP_K, n), i32),
                   jax.ShapeDtypeStruct((ne, 128), i32)],
        scratch_shapes=[pltpu.VMEM((ne, 1), f32), pltpu.VMEM((ne, 1), f32)],
        compiler_params=pltpu.CompilerParams(dimension_semantics=("arbitrary", "arbitrary"),
                                             vmem_limit_bytes=VMEM_LIMIT),
        name="route",
    )(logits_t)


def _dispatch_kernel(dest_ref, x_ref, xs_in_hbm, xs_hbm, sem, *, tt):
    del xs_in_hbm

    def row_copy(t, k):
        return pltpu.make_async_copy(x_ref.at[pl.ds(t, 1)], xs_hbm.at[pl.ds(dest_ref[k, t], 1)], sem)

    def issue(t, carry):
        for k in range(TOP_K):
            row_copy(t, k).start()
        return carry

    def drain(t, carry):
        for k in range(TOP_K):
            row_copy(t, k).wait()
        return carry

    lax.fori_loop(0, tt, issue, 0)
    lax.fori_loop(0, tt, drain, 0)


def _dispatch(dest, x1, n_slots):
    n, d = x1.shape
    tt = _tile(n, 512)
    xs0 = jnp.zeros((n_slots, d), f32)
    return pl.pallas_call(
        functools.partial(_dispatch_kernel, tt=tt),
        grid=(n // tt,),
        in_specs=[pl.BlockSpec((TOP_K, tt), lambda i: (0, i), memory_space=pltpu.SMEM),
                  pl.BlockSpec((tt, d), lambda i: (i, 0)),
                  pl.BlockSpec(memory_space=pl.ANY)],
        out_specs=pl.BlockSpec(memory_space=pl.ANY),
        out_shape=jax.ShapeDtypeStruct((n_slots, d), f32),
        scratch_shapes=[pltpu.SemaphoreType.DMA(())],
        input_output_aliases={2: 0},
        compiler_params=pltpu.CompilerParams(dimension_semantics=("arbitrary",), has_side_effects=True,
                                             vmem_limit_bytes=VMEM_LIMIT),
        name="dispatch",
    )(dest, x1, xs0)


def _expert_kernel(be_ref, nu_ref, xs_ref, wgu_ref, bgu_ref, wd_ref, bd_ref, ys_ref, wgu_c, wd_c):
    i = pl.program_id(0)

    @pl.when(i < nu_ref[0])
    def _():
        e = be_ref[i]
        prev = be_ref[jnp.maximum(i - 1, 0)]

        @pl.when((i == 0) | (e != prev))
        def _():
            wgu_c[...] = wgu_ref[0].astype(bf16)
            wd_c[...] = wd_ref[0].astype(bf16)

        x = xs_ref[...].astype(bf16)
        gu = jnp.dot(x, wgu_c[...], preferred_element_type=f32) + bgu_ref[0]
        gate = jnp.minimum(gu[:, :D_EXPERT], SWIGLU_LIMIT)
        up = jnp.clip(gu[:, D_EXPERT:], -SWIGLU_LIMIT, SWIGLU_LIMIT)
        act = (up + 1.0) * gate * jax.nn.sigmoid(SWIGLU_ALPHA * gate)
        ys_ref[...] = jnp.dot(act.astype(bf16), wd_c[...], preferred_element_type=f32) + bd_ref[0]

    @pl.when(i >= nu_ref[0])
    def _():
        ys_ref[...] = jnp.zeros_like(ys_ref)


def _expert_ffn(blk_expert, n_used, xs, w_gate_up, b_gate_up, w_down, b_down, bm):
    n_slots, d = xs.shape
    ne = w_gate_up.shape[0]
    n_blocks = n_slots // bm
    grid_spec = pltpu.PrefetchScalarGridSpec(
        num_scalar_prefetch=2,
        grid=(n_blocks,),
        in_specs=[pl.BlockSpec((bm, d), lambda i, be, nu: (i, 0)),
                  pl.BlockSpec((1, d, 2 * D_EXPERT), lambda i, be, nu: (be[i], 0, 0)),
                  pl.BlockSpec((1, 1, 2 * D_EXPERT), lambda i, be, nu: (be[i], 0, 0)),
                  pl.BlockSpec((1, D_EXPERT, d), lambda i, be, nu: (be[i], 0, 0)),
                  pl.BlockSpec((1, 1, d), lambda i, be, nu: (be[i], 0, 0))],
        out_specs=pl.BlockSpec((bm, d), lambda i, be, nu: (i, 0)),
        scratch_shapes=[pltpu.VMEM((d, 2 * D_EXPERT), bf16), pltpu.VMEM((D_EXPERT, d), bf16)],
    )
    return pl.pallas_call(
        _expert_kernel,
        grid_spec=grid_spec,
        out_shape=jax.ShapeDtypeStruct((n_slots, d), f32),
        compiler_params=pltpu.CompilerParams(dimension_semantics=("arbitrary",),
                                             vmem_limit_bytes=VMEM_LIMIT),
        name="expert_ffn",
    )(blk_expert, n_used, xs, w_gate_up, b_gate_up.reshape(ne, 1, 2 * D_EXPERT), w_down,
      b_down.reshape(ne, 1, d))


def _combine_kernel(dest_ref, ys_hbm, gate_ref, x1_ref, p_ref, g2_ref, b2_ref, wg_ref, bg_ref, wp_ref,
                    g3_ref, b3_ref, o_ref, buf_ref, sem, *, tt, alpha):
    def row_copy(t, k):
        return pltpu.make_async_copy(ys_hbm.at[pl.ds(dest_ref[k, t], 1)], buf_ref.at[k, pl.ds(t, 1)], sem)

    def issue(t, carry):
        for k in range(TOP_K):
            row_copy(t, k).start()
        return carry

    def drain(t, carry):
        for k in range(TOP_K):
            row_copy(t, k).wait()
        return carry

    lax.fori_loop(0, tt, issue, 0)
    lax.fori_loop(0, tt, drain, 0)

    gates = gate_ref[...]
    ffn = buf_ref[0] * gates[:, 0:1]
    for k in range(1, TOP_K):
        ffn = ffn + buf_ref[k] * gates[:, k:k + 1]
    x2 = _layer_norm(alpha * x1_ref[...] + ffn, g2_ref[...], b2_ref[...])
    gate = jax.nn.sigmoid(jnp.dot(x2.astype(bf16), wg_ref[...], preferred_element_type=f32) + bg_ref[...])
    emb = jnp.dot(p_ref[...].astype(bf16), wp_ref[...], preferred_element_type=f32)
    o_ref[...] = _layer_norm(alpha * x2 + gate * emb, g3_ref[...], b3_ref[...])


def _combine(dest, ys, gates_tok, x1, p2d, g2, b2, wg, bg, wp, g3, b3, alpha):
    n, d = x1.shape
    tt = _tile(n, 256)
    pd = p2d.shape[1]
    const = lambda i: (0, 0)
    return pl.pallas_call(
        functools.partial(_combine_kernel, tt=tt, alpha=alpha),
        grid=(n // tt,),
        in_specs=[pl.BlockSpec((TOP_K, tt), lambda i: (0, i), memory_space=pltpu.SMEM),
                  pl.BlockSpec(memory_space=pl.ANY),
                  pl.BlockSpec((tt, TOP_K), lambda i: (i, 0)),
                  pl.BlockSpec((tt, d), lambda i: (i, 0)),
                  pl.BlockSpec((tt, pd), lambda i: (i, 0)),
                  pl.BlockSpec((1, d), const),
                  pl.BlockSpec((1, d), const),
                  pl.BlockSpec((d, d), const),
                  pl.BlockSpec((1, d), const),
                  pl.BlockSpec((pd, d), const),
                  pl.BlockSpec((1, d), const),
                  pl.BlockSpec((1, d), const)],
        out_specs=pl.BlockSpec((tt, d), lambda i: (i, 0)),
        out_shape=jax.ShapeDtypeStruct((n, d), f32),
        scratch_shapes=[pltpu.VMEM((TOP_K, tt, d), f32), pltpu.SemaphoreType.DMA(())],
        compiler_params=pltpu.CompilerParams(dimension_semantics=("arbitrary",),
                                             vmem_limit_bytes=VMEM_LIMIT),
        name="combine_ln2_ple_ln3",
    )(dest, ys, gates_tok, x1, p2d, g2, b2, wg, bg, wp, g3, b3)


def _permute_in_proj(w):
    d = w.shape[0]
    o = 0
    gqkv = w[:, o:o + 3 * GDN_W]; o += 3 * GDN_W
    ga_gb = w[:, o:o + 2 * GDN_HEADS]; o += 2 * GDN_HEADS
    gz = w[:, o:o + GDN_W]; o += GDN_W
    dq = w[:, o:o + DIFF_W] * (DIFF_DH ** -0.5 * LOG2E); o += DIFF_W
    dkv = w[:, o:o + 2 * DIFF_W]
    pad = jnp.zeros((d, AB_W - 2 * GDN_HEADS), w.dtype)
    return jnp.concatenate([gqkv, gz, dq, dkv, ga_gb, pad], axis=1).astype(bf16)


def _lane_row(v, width):
    return jnp.zeros((1, width), f32).at[0, :v.shape[0]].set(v.astype(f32))


def kernel(x, p, w_in, conv_w, gdn_a_log, gdn_dt_bias, gdn_norm_w, diff_lq1, diff_lk1, diff_lq2, diff_lk2,
           diff_norm_w, w_out, rel_bias, ln1_g, ln1_b, router_w, router_b, w_gate_up, b_gate_up, w_down, b_down,
           ln2_g, ln2_b, ple_gate_w, ple_gate_b, ple_proj_w, ln3_g, ln3_b):
    batch, seq, d = x.shape
    depth = w_in.shape[0]
    n = batch * seq
    alpha = (2 * depth) ** 0.25
    bm = 512 if n * TOP_K >= 512 * N_EXPERTS else 128
    n_blocks = -(-n * TOP_K // bm) + N_EXPERTS
    tq = _tile(seq, ATTN_TILE)
    bias0, bias1 = _attn_bias_tiles(rel_bias, tq)
    row = lambda v: v.astype(f32).reshape(1, -1)

    x2d = x.reshape(n, d)
    for i in range(depth):
        lam_init = 0.8 - 0.6 * math.exp(-0.3 * i)
        gqkv, gz, dqkv, ab = _in_proj(x2d, _permute_in_proj(w_in[i]))
        o_gdn = _gdn(gqkv, ab, gz, conv_w[i].astype(f32), _lane_row(jnp.exp(gdn_a_log[i].astype(f32)), AB_W),
                     _lane_row(gdn_dt_bias[i], AB_W), row(gdn_norm_w[i]), batch, seq)
        lam = (jnp.exp(jnp.sum(diff_lq1[i].astype(f32) * diff_lk1[i].astype(f32)))
               - jnp.exp(jnp.sum(diff_lq2[i].astype(f32) * diff_lk2[i].astype(f32))) + lam_init)
        o_diff = _diff_attn(dqkv, bias0, bias1, diff_norm_w[i].astype(f32).reshape(-1, 1),
                            lam.reshape(1, 1).astype(f32), batch, seq, lam_init)
        w_o = w_out[i].astype(bf16)
        x1, logits_t = _out_proj(o_gdn, o_diff, x2d, w_o[:GDN_W], w_o[GDN_W:], row(ln1_g[i]), row(ln1_b[i]),
                                 jnp.transpose(router_w[i]).astype(f32), router_b[i].astype(f32).reshape(-1, 1), alpha)
        _, gates, dest, pend = _route(logits_t, bm)
        pend = pend[:, 0]
        blk_expert = jnp.minimum(
            jnp.sum((pend[None, :] <= (jnp.arange(n_blocks, dtype=i32) * bm)[:, None]).astype(i32), axis=1),
            N_EXPERTS - 1).astype(i32)
        n_used = (pend[N_EXPERTS - 1:] // bm).astype(i32)
        xs = _dispatch(dest, x1, n_blocks * bm)
        ys = _expert_ffn(blk_expert, n_used, xs, w_gate_up[i], b_gate_up[i], w_down[i], b_down[i], bm)
        x2d = _combine(dest, ys, jnp.transpose(gates), x1, p[i].reshape(n, -1), row(ln2_g[i]), row(ln2_b[i]),
                       ple_gate_w[i].astype(bf16), row(ple_gate_b[i]), ple_proj_w[i].astype(bf16),
                       row(ln3_g[i]), row(ln3_b[i]), alpha)
    return x2d.reshape(batch, seq, d)
```
